```python
import jax, jax.numpy as jnp
from jax import lax
import numpy as np

D_MODEL = 1024
BATCH = 16
SEQ = 2048
DEPTH = 4

N_META = 16
POOL_WINDOWS = (2, 4, 8, 16)
N_POOL_GROUPS = len(POOL_WINDOWS)
POOL_GROUP_DIM = D_MODEL // N_POOL_GROUPS
CONV_WIDTH = 3
D_FF = 2816
N_MIXERS = 2
N_POOL_LAYERS = (DEPTH + 1) // 2
N_CONV_LAYERS = DEPTH // 2
RMS_EPS = 1e-6

kernel_name = "hybrid_pool_shortconv_convffn_trunk"


def rms_norm(x, g):
    xf = x.astype(jnp.float32)
    y = xf * lax.rsqrt(jnp.mean(xf * xf, axis=-1, keepdims=True) + RMS_EPS)
    return (y * g.astype(jnp.float32)).astype(x.dtype)


def causal_dwconv3(x, w):
    L = x.shape[1]
    xp = jnp.pad(x, ((0, 0), (CONV_WIDTH - 1, 0), (0, 0)))
    return w[0] * xp[:, 0:L] + w[1] * xp[:, 1:L + 1] + w[2] * xp[:, 2:L + 2]


def pool_mixer(h, w_group, scale):
    Bsz, L, _ = h.shape
    hf = h.astype(jnp.float32).reshape(Bsz, L, N_POOL_GROUPS, POOL_GROUP_DIM)
    csum = jnp.cumsum(hf, axis=1)
    pos = jnp.arange(L, dtype=jnp.float32)
    outs = []
    for g, w in enumerate(POOL_WINDOWS):
        cg = csum[:, :, g]
        prev = jnp.pad(cg, ((0, 0), (w, 0), (0, 0)))[:, :L]
        count = jnp.minimum(pos + 1.0, float(w))[None, :, None]
        outs.append((cg - prev) / count - hf[:, :, g])
    pooled = jnp.stack(outs, axis=2).astype(h.dtype)
    y = jnp.einsum('blgc,gcd->blgd', pooled, w_group).reshape(Bsz, L, D_MODEL)
    return y * scale


def short_conv_mixer(h, w_in, conv_w, w_out):
    bcv = jnp.einsum('bld,de->ble', h, w_in)
    b_gate, c_gate, v = jnp.split(bcv, 3, axis=-1)
    y = b_gate * causal_dwconv3(c_gate * v, conv_w)
    return jnp.einsum('bld,de->ble', y, w_out)


def conv_ffn(h, w_up, conv_w, w_down):
    up = jnp.einsum('bld,df->blf', h, w_up)
    gate, val = jnp.split(up, 2, axis=-1)
    gate = causal_dwconv3(gate, conv_w)
    return jnp.einsum('blf,fd->bld', jax.nn.silu(gate) * val, w_down)


def setup_inputs(seed: int = 0) -> dict:
    key = jax.random.key(seed)
    ks = jax.random.split(key, 12)
    f32 = jnp.float32
    D = D_MODEL
    x = jax.random.normal(ks[0], (BATCH, SEQ, D), f32)
    meta_tokens = jax.random.normal(ks[1], (N_META, D), f32)
    pool_w = jax.random.normal(ks[2], (N_POOL_LAYERS, N_POOL_GROUPS, POOL_GROUP_DIM, POOL_GROUP_DIM), f32) * POOL_GROUP_DIM ** -0.5
    pool_scale = 1.0 + 0.1 * jax.random.normal(ks[3], (N_POOL_LAYERS, D), f32)
    sc_w_in = jax.random.normal(ks[4], (N_CONV_LAYERS, D, 3 * D), f32) * D ** -0.5
    sc_conv = jax.random.normal(ks[5], (N_CONV_LAYERS, CONV_WIDTH, D), f32) * CONV_WIDTH ** -0.5
    sc_w_out = jax.random.normal(ks[6], (N_CONV_LAYERS, D, D), f32) * D ** -0.5
    ffn_w_up = jax.random.normal(ks[7], (DEPTH, D, 2 * D_FF), f32) * D ** -0.5
    ffn_conv = jax.random.normal(ks[8], (DEPTH, CONV_WIDTH, D_FF), f32) * CONV_WIDTH ** -0.5
    ffn_w_down = jax.random.normal(ks[9], (DEPTH, D_FF, D), f32) * D_FF ** -0.5
    norm_g = 1.0 + 0.05 * jax.random.normal(ks[10], (DEPTH, 4, D), f32)
    return {"x": x, "meta_tokens": meta_tokens, "pool_w": pool_w, "pool_scale": pool_scale,
            "sc_w_in": sc_w_in, "sc_conv": sc_conv, "sc_w_out": sc_w_out,
            "ffn_w_up": ffn_w_up, "ffn_conv": ffn_conv, "ffn_w_down": ffn_w_down,
            "norm_g": norm_g}


def reference(x, meta_tokens, pool_w, pool_scale, sc_w_in, sc_conv, sc_w_out,
              ffn_w_up, ffn_conv, ffn_w_down, norm_g):
    Bsz = x.shape[0]
    meta = jnp.broadcast_to(meta_tokens.astype(x.dtype)[None], (Bsz, N_META, D_MODEL))
    h = jnp.concatenate([meta, x], axis=1)
    for i in range(DEPTH):
        j = i // N_MIXERS
        u = rms_norm(h, norm_g[i, 0])
        if i % N_MIXERS == 0:
            m = pool_mixer(u, pool_w[j], pool_scale[j])
        else:
            m = short_conv_mixer(u, sc_w_in[j], sc_conv[j], sc_w_out[j])
        h = h + rms_norm(m, norm_g[i, 1])
        u = rms_norm(h, norm_g[i, 2])
        f = conv_ffn(u, ffn_w_up[i], ffn_conv[i], ffn_w_down[i])
        h = h + rms_norm(f, norm_g[i, 3])
    return h[:, N_META:]
```

```python
import functools

import jax
import jax.numpy as jnp
from jax import lax
from jax.experimental import pallas as pl
from jax.experimental.pallas import tpu as pltpu

D_MODEL = 1024
N_META = 16
POOL_WINDOWS = (2, 4, 8, 16)
POOL_GROUP_DIM = D_MODEL // len(POOL_WINDOWS)
D_FF = 2816
RMS_EPS = 1e-6

SUBLANES = 8
CONV_HIST = SUBLANES
POOL_HIST = max(POOL_WINDOWS)
FF_CHUNK = 256
MIX_CHUNK = 256
SEQ_TILE = 512
VMEM_LIMIT_BYTES = 56 * 1024 * 1024

_BF16 = jnp.bfloat16
_F32 = jnp.float32


def _dot(a, b):
    return jnp.dot(a, b, preferred_element_type=_F32)


def _rms(x, g):
    ms = jnp.mean(x * x, axis=-1, keepdims=True)
    return x * lax.rsqrt(ms + RMS_EPS) * g


def _pool_mixer(hin, g0, g1, pw_ref, ps_ref, ubuf, rows, pos0):
    u = _rms(hin, g0)
    ubuf[POOL_HIST:POOL_HIST + rows, :] = u
    outs = []
    for g, w in enumerate(POOL_WINDOWS):
        lanes = slice(g * POOL_GROUP_DIM, (g + 1) * POOL_GROUP_DIM)
        ug = u[:, lanes]
        s = ug
        for k in range(1, w):
            s = s + ubuf[POOL_HIST - k:POOL_HIST - k + rows, lanes]
        if pos0 is None:
            mean = s * (1.0 / w)
        else:
            pos = pos0 + lax.broadcasted_iota(jnp.int32, (rows, 1), 0)
            mean = s / jnp.minimum(pos + 1, w).astype(_F32)
        outs.append(_dot((mean - ug).astype(_BF16), pw_ref[g]))
    y = jnp.concatenate(outs, axis=1) * ps_ref[...]
    ubuf[0:POOL_HIST, :] = ubuf[rows:rows + POOL_HIST, :]
    return hin + _rms(y, g1)


def _conv_mixer(hin, g0, g1, win_ref, cw_ref, wout_ref, cvbuf, ybuf, rows):
    u = _rms(hin, g0).astype(_BF16)
    for c in range(D_MODEL // MIX_CHUNK):
        lo = c * MIX_CHUNK
        cols = slice(lo, lo + MIX_CHUNK)
        b = _dot(u, win_ref[:, lo:lo + MIX_CHUNK])
        cg = _dot(u, win_ref[:, D_MODEL + lo:D_MODEL + lo + MIX_CHUNK])
        v = _dot(u, win_ref[:, 2 * D_MODEL + lo:2 * D_MODEL + lo + MIX_CHUNK])
        cv = cg * v
        cvbuf[CONV_HIST:CONV_HIST + rows, cols] = cv
        conv = (cw_ref[0:1, cols] * cvbuf[CONV_HIST - 2:CONV_HIST - 2 + rows, cols]
                + cw_ref[1:2, cols] * cvbuf[CONV_HIST - 1:CONV_HIST - 1 + rows, cols]
                + cw_ref[2:3, cols] * cv)
        ybuf[:, cols] = (b * conv).astype(_BF16)
    m = _dot(ybuf[...], wout_ref[...])
    cvbuf[0:CONV_HIST, :] = cvbuf[rows:rows + CONV_HIST, :]
    return hin + _rms(m, g1)


def _conv_ffn(h1, g2, g3, wup_ref, fw_ref, wdown_ref, gbuf, rows):
    u = _rms(h1, g2).astype(_BF16)
    acc = jnp.zeros((rows, D_MODEL), _F32)
    for c in range(D_FF // FF_CHUNK):
        lo = c * FF_CHUNK
        cols = slice(lo, lo + FF_CHUNK)
        gate = _dot(u, wup_ref[:, lo:lo + FF_CHUNK])
        val = _dot(u, wup_ref[:, D_FF + lo:D_FF + lo + FF_CHUNK])
        gbuf[CONV_HIST:CONV_HIST + rows, cols] = gate
        conv = (fw_ref[0:1, cols] * gbuf[CONV_HIST - 2:CONV_HIST - 2 + rows, cols]
                + fw_ref[1:2, cols] * gbuf[CONV_HIST - 1:CONV_HIST - 1 + rows, cols]
                + fw_ref[2:3, cols] * gate)
        act = (conv * jax.nn.sigmoid(conv) * val).astype(_BF16)
        acc = acc + _dot(act, wdown_ref[lo:lo + FF_CHUNK, :])
    gbuf[0:CONV_HIST, :] = gbuf[rows:rows + CONV_HIST, :]
    return h1 + _rms(acc, g3)


def _layer_kernel(*refs, kind, rows, count_rows, emit_carry):
    if kind == "pool":
        (h_ref, g_ref, pw_ref, ps_ref, wup_ref, fw_ref, wdown_ref, mhist_in, ghist_in), rest = refs[:9], refs[9:]
    else:
        (h_ref, g_ref, win_ref, cw_ref, wout_ref, wup_ref, fw_ref, wdown_ref, mhist_in, ghist_in), rest = (
            refs[:10], refs[10:])
    if emit_carry:
        out_ref, mhist_out, ghist_out = rest[:3]
        scratch = rest[3:]
    else:
        out_ref = rest[0]
        scratch = rest[1:]

    t = pl.program_id(1)
    mbuf, gbuf = scratch[0], scratch[1]
    mhist_rows = mhist_in.shape[0]

    @pl.when(t == 0)
    def _():
        mbuf[0:mhist_rows, :] = mhist_in[...]
        gbuf[0:CONV_HIST, :] = ghist_in[...]

    hin = h_ref[0]
    g0, g1, g2, g3 = (g_ref[i:i + 1, :] for i in range(4))
    if kind == "pool":
        pos0 = t * rows if count_rows else None
        h1 = _pool_mixer(hin, g0, g1, pw_ref, ps_ref, mbuf, rows, pos0)
    else:
        h1 = _conv_mixer(hin, g0, g1, win_ref, cw_ref, wout_ref, mbuf, scratch[2], rows)
    out_ref[0] = _conv_ffn(h1, g2, g3, wup_ref, fw_ref, wdown_ref, gbuf, rows)

    if emit_carry:
        mhist_out[...] = mbuf[0:mhist_rows, :]
        ghist_out[...] = gbuf[0:CONV_HIST, :]


def _resident(arr):
    nd = arr.ndim
    return pl.BlockSpec(arr.shape, lambda b, t, _nd=nd: (0,) * _nd, pipeline_mode=pl.Buffered(1))


def _layer_call(kind, h, weights, mhist, ghist, *, rows, count_rows, emit_carry):
    bsz, seq, _ = h.shape
    assert seq % rows == 0
    mhist_rows = mhist.shape[0]
    h_spec = pl.BlockSpec((1, rows, D_MODEL), lambda b, t: (b, t, 0))
    in_specs = [h_spec] + [_resident(w) for w in weights] + [_resident(mhist), _resident(ghist)]
    out_shape = [jax.ShapeDtypeStruct(h.shape, _F32)]
    out_specs = [h_spec]
    if emit_carry:
        out_shape += [jax.ShapeDtypeStruct(mhist.shape, _F32), jax.ShapeDtypeStruct(ghist.shape, _F32)]
        out_specs += [pl.BlockSpec(mhist.shape, lambda b, t: (0, 0)), pl.BlockSpec(ghist.shape, lambda b, t: (0, 0))]
    scratch = [pltpu.VMEM((mhist_rows + rows, D_MODEL), _F32),
               pltpu.VMEM((CONV_HIST + rows, D_FF), _F32)]
    if kind == "conv":
        scratch.append(pltpu.VMEM((rows, D_MODEL), _BF16))
    out = pl.pallas_call(
        functools.partial(_layer_kernel, kind=kind, rows=rows, count_rows=count_rows, emit_carry=emit_carry),
        grid=(bsz, seq // rows),
        in_specs=in_specs,
        out_specs=out_specs,
        out_shape=out_shape,
        scratch_shapes=scratch,
        compiler_params=pltpu.CompilerParams(
            dimension_semantics=("arbitrary", "arbitrary"),
            vmem_limit_bytes=VMEM_LIMIT_BYTES),
    )(h, *weights, mhist, ghist)
    return out if emit_carry else out[0]


def kernel(x, meta_tokens, pool_w, pool_scale, sc_w_in, sc_conv, sc_w_out, ffn_w_up, ffn_conv, ffn_w_down, norm_g):
    depth = norm_g.shape[0]
    hm = meta_tokens.astype(_F32)[None]
    hx = x
    for i in range(depth):
        j = i // 2
        ffn = (ffn_w_up[i].astype(_BF16), ffn_conv[i], ffn_w_down[i].astype(_BF16))
        if i % 2 == 0:
            kind = "pool"
            weights = (norm_g[i], pool_w[j].astype(_BF16), pool_scale[j][None]) + ffn
            mhist0 = jnp.zeros((POOL_HIST, D_MODEL), _F32)
        else:
            kind = "conv"
            weights = (norm_g[i], sc_w_in[j].astype(_BF16), sc_conv[j], sc_w_out[j].astype(_BF16)) + ffn
            mhist0 = jnp.zeros((CONV_HIST, D_MODEL), _F32)
        ghist0 = jnp.zeros((CONV_HIST, D_FF), _F32)
        hm, mhist, ghist = _layer_call(kind, hm, weights, mhist0, ghist0,
                                       rows=N_META, count_rows=True, emit_carry=True)
        hx = _layer_call(kind, hx, weights, mhist, ghist,
                         rows=SEQ_TILE, count_rows=False, emit_carry=False)
    return hx
```

```python
import functools

import jax
import jax.numpy as jnp
from jax import lax
from jax.experimental import pallas as pl
from jax.experimental.pallas import tpu as pltpu

D_MODEL = 1024
N_META = 16
POOL_WINDOWS = (2, 4, 8, 16)
POOL_GROUP_DIM = D_MODEL // len(POOL_WINDOWS)
D_FF = 2816
RMS_EPS = 1e-6

SUBLANES = 8
CONV_HIST = SUBLANES
POOL_HIST = max(POOL_WINDOWS)
FF_CHUNK = 256
MIX_CHUNK = 256
SEQ_TILE = 512
VMEM_LIMIT_BYTES = 56 * 1024 * 1024

_BF16 = jnp.bfloat16
_F32 = jnp.float32


def _dot(a, b):
    return jnp.dot(a, b, preferred_element_type=_F32)


def _rms(x, g):
    ms = jnp.mean(x * x, axis=-1, keepdims=True)
    return x * lax.rsqrt(ms + RMS_EPS) * g


def _pool_mixer(hin, g0, g1, pw_ref, ps_ref, ubuf, rows, pos0):
    u = _rms(hin, g0)
    ubuf[POOL_HIST:POOL_HIST + rows, :] = u
    outs = []
    for g, w in enumerate(POOL_WINDOWS):
        lanes = slice(g * POOL_GROUP_DIM, (g + 1) * POOL_GROUP_DIM)
        ug = u[:, lanes]
        s = ug
        for k in range(1, w):
            s = s + ubuf[POOL_HIST - k:POOL_HIST - k + rows, lanes]
        if pos0 is None:
            mean = s * (1.0 / w)
        else:
            pos = pos0 + lax.broadcasted_iota(jnp.int32, (rows, 1), 0)
            mean = s / jnp.minimum(pos + 1, w).astype(_F32)
        outs.append(_dot((mean - ug).astype(_BF16), pw_ref[g]))
    y = jnp.concatenate(outs, axis=1) * ps_ref[...]
    ubuf[0:POOL_HIST, :] = ubuf[rows:rows + POOL_HIST, :]
    return hin + _rms(y, g1)


def _conv_mixer(hin, g0, g1, win_ref, cw_ref, wout_ref, cvbuf, ybuf, rows):
    u = _rms(hin, g0).astype(_BF16)
    for c in range(D_MODEL // MIX_CHUNK):
        lo = c * MIX_CHUNK
        cols = slice(lo, lo + MIX_CHUNK)
        b = _dot(u, win_ref[:, lo:lo + MIX_CHUNK])
        cg = _dot(u, win_ref[:, D_MODEL + lo:D_MODEL + lo + MIX_CHUNK])
        v = _dot(u, win_ref[:, 2 * D_MODEL + lo:2 * D_MODEL + lo + MIX_CHUNK])
        cv = cg * v
        cvbuf[CONV_HIST:CONV_HIST + rows, cols] = cv
        conv = (cw_ref[0:1, cols] * cvbuf[CONV_HIST - 2:CONV_HIST - 2 + rows, cols]
                + cw_ref[1:2, cols] * cvbuf[CONV_HIST - 1:CONV_HIST - 1 + rows, cols]
                + cw_ref[2:3, cols] * cv)
        ybuf[:, cols] = (b * conv).astype(_BF16)
    m = _dot(ybuf[...], wout_ref[...])
    cvbuf[0:CONV_HIST, :] = cvbuf[rows:rows + CONV_HIST, :]
    return hin + _rms(m, g1)


def _conv_ffn(h1, g2, g3, wup_ref, fw_ref, wdown_ref, gbuf, abuf, rows):
    u = _rms(h1, g2).astype(_BF16)
    for c in range(D_FF // FF_CHUNK):
        lo = c * FF_CHUNK
        cols = slice(lo, lo + FF_CHUNK)
        gate = _dot(u, wup_ref[:, lo:lo + FF_CHUNK])
        val = _dot(u, wup_ref[:, D_FF + lo:D_FF + lo + FF_CHUNK])
        gbuf[CONV_HIST:CONV_HIST + rows, cols] = gate
        conv = (fw_ref[0:1, cols] * gbuf[CONV_HIST - 2:CONV_HIST - 2 + rows, cols]
                + fw_ref[1:2, cols] * gbuf[CONV_HIST - 1:CONV_HIST - 1 + rows, cols]
                + fw_ref[2:3, cols] * gate)
        abuf[:, cols] = (conv * jax.nn.sigmoid(conv) * val).astype(_BF16)
    gbuf[0:CONV_HIST, :] = gbuf[rows:rows + CONV_HIST, :]
    f = _dot(abuf[...], wdown_ref[...])
    return h1 + _rms(f, g3)


def _layer_kernel(*refs, kind, rows, count_rows, emit_carry):
    if kind == "pool":
        (h_ref, g_ref, pw_ref, ps_ref, wup_ref, fw_ref, wdown_ref, mhist_in, ghist_in), rest = refs[:9], refs[9:]
    else:
        (h_ref, g_ref, win_ref, cw_ref, wout_ref, wup_ref, fw_ref, wdown_ref, mhist_in, ghist_in), rest = (
            refs[:10], refs[10:])
    if emit_carry:
        out_ref, mhist_out, ghist_out = rest[:3]
        scratch = rest[3:]
    else:
        out_ref = rest[0]
        scratch = rest[1:]

    t = pl.program_id(1)
    mbuf, gbuf, abuf = scratch[0], scratch[1], scratch[2]
    mhist_rows = mhist_in.shape[0]

    @pl.when(t == 0)
    def _():
        mbuf[0:mhist_rows, :] = mhist_in[...]
        gbuf[0:CONV_HIST, :] = ghist_in[...]

    hin = h_ref[0]
    g0, g1, g2, g3 = (g_ref[i:i + 1, :] for i in range(4))
    if kind == "pool":
        pos0 = t * rows if count_rows else None
        h1 = _pool_mixer(hin, g0, g1, pw_ref, ps_ref, mbuf, rows, pos0)
    else:
        h1 = _conv_mixer(hin, g0, g1, win_ref, cw_ref, wout_ref, mbuf, scratch[3], rows)
    out_ref[0] = _conv_ffn(h1, g2, g3, wup_ref, fw_ref, wdown_ref, gbuf, abuf, rows)

    if emit_carry:
        mhist_out[...] = mbuf[0:mhist_rows, :]
        ghist_out[...] = gbuf[0:CONV_HIST, :]


def _resident(arr):
    nd = arr.ndim
    return pl.BlockSpec(arr.shape, lambda b, t, _nd=nd: (0,) * _nd, pipeline_mode=pl.Buffered(1))


def _layer_call(kind, h, weights, mhist, ghist, *, rows, count_rows, emit_carry):
    bsz, seq, _ = h.shape
    assert seq % rows == 0
    mhist_rows = mhist.shape[0]
    h_spec = pl.BlockSpec((1, rows, D_MODEL), lambda b, t: (b, t, 0))
    in_specs = [h_spec] + [_resident(w) for w in weights] + [_resident(mhist), _resident(ghist)]
    out_shape = [jax.ShapeDtypeStruct(h.shape, _F32)]
    out_specs = [h_spec]
    if emit_carry:
        out_shape += [jax.ShapeDtypeStruct(mhist.shape, _F32), jax.ShapeDtypeStruct(ghist.shape, _F32)]
        out_specs += [pl.BlockSpec(mhist.shape, lambda b, t: (0, 0)), pl.BlockSpec(ghist.shape, lambda b, t: (0, 0))]
    scratch = [pltpu.VMEM((mhist_rows + rows, D_MODEL), _F32),
               pltpu.VMEM((CONV_HIST + rows, D_FF), _F32),
               pltpu.VMEM((rows, D_FF), _BF16)]
    if kind == "conv":
        scratch.append(pltpu.VMEM((rows, D_MODEL), _BF16))
    out = pl.pallas_call(
        functools.partial(_layer_kernel, kind=kind, rows=rows, count_rows=count_rows, emit_carry=emit_carry),
        grid=(bsz, seq // rows),
        in_specs=in_specs,
        out_specs=out_specs,
        out_shape=out_shape,
        scratch_shapes=scratch,
        compiler_params=pltpu.CompilerParams(
            dimension_semantics=("arbitrary", "arbitrary"),
            vmem_limit_bytes=VMEM_LIMIT_BYTES),
    )(h, *weights, mhist, ghist)
    return out if emit_carry else out[0]


def kernel(x, meta_tokens, pool_w, pool_scale, sc_w_in, sc_conv, sc_w_out, ffn_w_up, ffn_conv, ffn_w_down, norm_g):
    depth = norm_g.shape[0]
    hm = meta_tokens.astype(_F32)[None]
    hx = x
    for i in range(depth):
        j = i // 2
        ffn = (ffn_w_up[i].astype(_BF16), ffn_conv[i], ffn_w_down[i].astype(_BF16))
        if i % 2 == 0:
            kind = "pool"
            weights = (norm_g[i], pool_w[j].astype(_BF16), pool_scale[j][None]) + ffn
            mhist0 = jnp.zeros((POOL_HIST, D_MODEL), _F32)
        else:
            kind = "conv"
            weights = (norm_g[i], sc_w_in[j].astype(_BF16), sc_conv[j], sc_w_out[j].astype(_BF16)) + ffn
            mhist0 = jnp.zeros((CONV_HIST, D_MODEL), _F32)
        ghist0 = jnp.zeros((CONV_HIST, D_FF), _F32)
        hm, mhist, ghist = _layer_call(kind, hm, weights, mhist0, ghist0,
                                       rows=N_META, count_rows=True, emit_carry=True)
        hx = _layer_call(kind, hx, weights, mhist, ghist,
                         rows=SEQ_TILE, count_rows=False, emit_carry=False)
    return hx
```

```python
import functools

import jax
import jax.numpy as jnp
from jax import lax
from jax.experimental import pallas as pl
from jax.experimental.pallas import tpu as pltpu

D_MODEL = 1024
N_META = 16
POOL_WINDOWS = (2, 4, 8, 16)
POOL_GROUP_DIM = D_MODEL // len(POOL_WINDOWS)
D_FF = 2816
RMS_EPS = 1e-6

SUBLANES = 8
CONV_HIST = SUBLANES
POOL_HIST = max(POOL_WINDOWS)
FF_CHUNK = 256
MIX_CHUNK = 256
SEQ_TILE = 512
VMEM_LIMIT_BYTES = 56 * 1024 * 1024

_BF16 = jnp.bfloat16
_F32 = jnp.float32


def _dot(a, b):
    return jnp.dot(a, b, preferred_element_type=_F32)


def _rms(x, g):
    ms = jnp.mean(x * x, axis=-1, keepdims=True)
    return x * lax.rsqrt(ms + RMS_EPS) * g


def _pool_mixer(hin, g0, g1, pw_ref, ps_ref, hist, rows, pos0):
    u = _rms(hin, g0)
    part = jnp.concatenate([hist[...], u], axis=0)
    hist[...] = u[rows - POOL_HIST:, :]
    outs = []
    for g, w in enumerate(POOL_WINDOWS):
        part = part + pltpu.roll(part, w // 2, axis=0)
        s = part[POOL_HIST:, :POOL_GROUP_DIM]
        part = part[:, POOL_GROUP_DIM:]
        ug = u[:, g * POOL_GROUP_DIM:(g + 1) * POOL_GROUP_DIM]
        if pos0 is None:
            mean = s * (1.0 / w)
        else:
            pos = pos0 + lax.broadcasted_iota(jnp.int32, (rows, 1), 0)
            mean = s / jnp.minimum(pos + 1, w).astype(_F32)
        outs.append(_dot((mean - ug).astype(_BF16), pw_ref[g]))
    y = jnp.concatenate(outs, axis=1) * ps_ref[...]
    return hin + _rms(y, g1)


def _conv_mixer(hin, g0, g1, win_ref, cw_ref, wout_ref, cvbuf, ybuf, rows):
    u = _rms(hin, g0).astype(_BF16)
    for c in range(D_MODEL // MIX_CHUNK):
        lo = c * MIX_CHUNK
        cols = slice(lo, lo + MIX_CHUNK)
        b = _dot(u, win_ref[:, lo:lo + MIX_CHUNK])
        cg = _dot(u, win_ref[:, D_MODEL + lo:D_MODEL + lo + MIX_CHUNK])
        v = _dot(u, win_ref[:, 2 * D_MODEL + lo:2 * D_MODEL + lo + MIX_CHUNK])
        cv = cg * v
        cvbuf[CONV_HIST:CONV_HIST + rows, cols] = cv
        conv = (cw_ref[0:1, cols] * cvbuf[CONV_HIST - 2:CONV_HIST - 2 + rows, cols]
                + cw_ref[1:2, cols] * cvbuf[CONV_HIST - 1:CONV_HIST - 1 + rows, cols]
                + cw_ref[2:3, cols] * cv)
        ybuf[:, cols] = (b * conv).astype(_BF16)
    m = _dot(ybuf[...], wout_ref[...])
    cvbuf[0:CONV_HIST, :] = cvbuf[rows:rows + CONV_HIST, :]
    return hin + _rms(m, g1)


def _conv_ffn(h1, g2, g3, wup_ref, fw_ref, wdown_ref, gbuf, abuf, rows):
    u = _rms(h1, g2).astype(_BF16)
    for c in range(D_FF // FF_CHUNK):
        lo = c * FF_CHUNK
        cols = slice(lo, lo + FF_CHUNK)
        gate = _dot(u, wup_ref[:, lo:lo + FF_CHUNK])
        val = _dot(u, wup_ref[:, D_FF + lo:D_FF + lo + FF_CHUNK])
        gbuf[CONV_HIST:CONV_HIST + rows, cols] = gate
        conv = (fw_ref[0:1, cols] * gbuf[CONV_HIST - 2:CONV_HIST - 2 + rows, cols]
                + fw_ref[1:2, cols] * gbuf[CONV_HIST - 1:CONV_HIST - 1 + rows, cols]
                + fw_ref[2:3, cols] * gate)
        abuf[:, cols] = (conv * jax.nn.sigmoid(conv) * val).astype(_BF16)
    gbuf[0:CONV_HIST, :] = gbuf[rows:rows + CONV_HIST, :]
    f = _dot(abuf[...], wdown_ref[...])
    return h1 + _rms(f, g3)


def _layer_kernel(*refs, kind, rows, count_rows, emit_carry):
    if kind == "pool":
        (h_ref, g_ref, pw_ref, ps_ref, wup_ref, fw_ref, wdown_ref, mhist_in, ghist_in), rest = refs[:9], refs[9:]
    else:
        (h_ref, g_ref, win_ref, cw_ref, wout_ref, wup_ref, fw_ref, wdown_ref, mhist_in, ghist_in), rest = (
            refs[:10], refs[10:])
    if emit_carry:
        out_ref, mhist_out, ghist_out = rest[:3]
        scratch = rest[3:]
    else:
        out_ref = rest[0]
        scratch = rest[1:]

    t = pl.program_id(1)
    mbuf, gbuf, abuf = scratch[0], scratch[1], scratch[2]
    mhist_rows = mhist_in.shape[0]

    @pl.when(t == 0)
    def _():
        mbuf[0:mhist_rows, :] = mhist_in[...]
        gbuf[0:CONV_HIST, :] = ghist_in[...]

    hin = h_ref[0]
    g0, g1, g2, g3 = (g_ref[i:i + 1, :] for i in range(4))
    if kind == "pool":
        pos0 = t * rows if count_rows else None
        h1 = _pool_mixer(hin, g0, g1, pw_ref, ps_ref, mbuf, rows, pos0)
    else:
        h1 = _conv_mixer(hin, g0, g1, win_ref, cw_ref, wout_ref, mbuf, scratch[3], rows)
    out_ref[0] = _conv_ffn(h1, g2, g3, wup_ref, fw_ref, wdown_ref, gbuf, abuf, rows)

    if emit_carry:
        mhist_out[...] = mbuf[0:mhist_rows, :]
        ghist_out[...] = gbuf[0:CONV_HIST, :]


def _resident(arr):
    nd = arr.ndim
    return pl.BlockSpec(arr.shape, lambda b, t, _nd=nd: (0,) * _nd, pipeline_mode=pl.Buffered(1))


def _layer_call(kind, h, weights, mhist, ghist, *, rows, count_rows, emit_carry):
    bsz, seq, _ = h.shape
    assert seq % rows == 0
    mhist_rows = mhist.shape[0]
    h_spec = pl.BlockSpec((1, rows, D_MODEL), lambda b, t: (b, t, 0))
    in_specs = [h_spec] + [_resident(w) for w in weights] + [_resident(mhist), _resident(ghist)]
    out_shape = [jax.ShapeDtypeStruct(h.shape, _F32)]
    out_specs = [h_spec]
    if emit_carry:
        out_shape += [jax.ShapeDtypeStruct(mhist.shape, _F32), jax.ShapeDtypeStruct(ghist.shape, _F32)]
        out_specs += [pl.BlockSpec(mhist.shape, lambda b, t: (0, 0)), pl.BlockSpec(ghist.shape, lambda b, t: (0, 0))]
    mixer_rows = mhist_rows if kind == "pool" else mhist_rows + rows
    scratch = [pltpu.VMEM((mixer_rows, D_MODEL), _F32),
               pltpu.VMEM((CONV_HIST + rows, D_FF), _F32),
               pltpu.VMEM((rows, D_FF), _BF16)]
    if kind == "conv":
        scratch.append(pltpu.VMEM((rows, D_MODEL), _BF16))
    out = pl.pallas_call(
        functools.partial(_layer_kernel, kind=kind, rows=rows, count_rows=count_rows, emit_carry=emit_carry),
        grid=(bsz, seq // rows),
        in_specs=in_specs,
        out_specs=out_specs,
        out_shape=out_shape,
        scratch_shapes=scratch,
        compiler_params=pltpu.CompilerParams(
            dimension_semantics=("arbitrary", "arbitrary"),
            vmem_limit_bytes=VMEM_LIMIT_BYTES),
    )(h, *weights, mhist, ghist)
    return out if emit_carry else out[0]


def kernel(x, meta_tokens, pool_w, pool_scale, sc_w_in, sc_conv, sc_w_out, ffn_w_up, ffn_conv, ffn_w_down, norm_g):
    depth = norm_g.shape[0]
    hm = meta_tokens.astype(_F32)[None]
    hx = x
    for i in range(depth):
        j = i // 2
        ffn = (ffn_w_up[i].astype(_BF16), ffn_conv[i], ffn_w_down[i].astype(_BF16))
        if i % 2 == 0:
            kind = "pool"
            weights = (norm_g[i], pool_w[j].astype(_BF16), pool_scale[j][None]) + ffn
            mhist0 = jnp.zeros((POOL_HIST, D_MODEL), _F32)
        else:
            kind = "conv"
            weights = (norm_g[i], sc_w_in[j].astype(_BF16), sc_conv[j], sc_w_out[j].astype(_BF16)) + ffn
            mhist0 = jnp.zeros((CONV_HIST, D_MODEL), _F32)
        ghist0 = jnp.zeros((CONV_HIST, D_FF), _F32)
        hm, mhist, ghist = _layer_call(kind, hm, weights, mhist0, ghist0,
                                       rows=N_META, count_rows=True, emit_carry=True)
        hx = _layer_call(kind, hx, weights, mhist, ghist,
                         rows=SEQ_TILE, count_rows=False, emit_carry=False)
    return hx
```
